```python
import jax, jax.numpy as jnp
from jax import lax
import numpy as np

D_MODEL = 2048
BATCH = 2
SEQ = 4096
DEPTH = 1

CHUNK = 64
D_FF = 5632
W_CONV = D_MODEL // 2
CONV_K = 31
W_POOL = D_MODEL // 2
POOL_WINDOWS = (2, 4, 8, 16)
N_POOL_GROUPS = len(POOL_WINDOWS)
POOL_GROUP_IN = W_POOL // N_POOL_GROUPS
POOL_GROUP_OUT = D_MODEL // N_POOL_GROUPS
N_BRANCHES = 2
N_IN_COLS = 2 * W_CONV + W_POOL + N_BRANCHES * D_MODEL
N_ADA = 3
EPS = 1e-6

kernel_name = "hybrid_conformer_pool_gated_block"


def rms_norm(x, g):
    xf = x.astype(jnp.float32)
    y = xf * lax.rsqrt(jnp.mean(xf * xf, axis=-1, keepdims=True) + EPS)
    return y.astype(x.dtype) * g


def layer_norm(x, g, b):
    xf = x.astype(jnp.float32)
    mu = jnp.mean(xf, axis=-1, keepdims=True)
    var = jnp.mean(jnp.square(xf - mu), axis=-1, keepdims=True)
    y = (xf - mu) * lax.rsqrt(var + EPS)
    return y.astype(x.dtype) * g + b


def modulate(n, shift, scale):
    return n * (1.0 + scale[:, None, :]) + shift[:, None, :]


def swiglu(n, w_in, w_out):
    hu = n @ w_in
    h, u = jnp.split(hu, 2, axis=-1)
    return (jax.nn.silu(h) * u) @ w_out


def causal_mean_pool(v, window):
    seq = v.shape[1]
    vf = v.astype(jnp.float32)
    csum = jnp.cumsum(vf, axis=1)
    lagged = jnp.pad(csum, ((0, 0), (window, 0), (0, 0)))[:, :seq]
    count = jnp.minimum(jnp.arange(1, seq + 1, dtype=jnp.float32), float(window))
    return ((csum - lagged) / count[None, :, None]).astype(v.dtype)


def conformer_conv_branch(glu_in, conv_w, conv_b, ln_a_g, ln_a_b, w_a_out, b_a_out):
    a, g = jnp.split(glu_in, 2, axis=-1)
    a = a * jax.nn.sigmoid(g)
    a = lax.conv_general_dilated(
        a, conv_w[:, None, :], window_strides=(1,), padding=[(CONV_K - 1, 0)],
        dimension_numbers=('NWC', 'WIO', 'NWC'), feature_group_count=W_CONV) + conv_b
    a = jax.nn.silu(layer_norm(a, ln_a_g, ln_a_b))
    return a @ w_a_out + b_a_out


def pool_branch(v, w_b_group, b_b_group, ls_b):
    bsz, seq, _ = v.shape
    groups = jnp.split(v, N_POOL_GROUPS, axis=-1)
    mixed = jnp.stack([causal_mean_pool(vg, w) - vg for vg, w in zip(groups, POOL_WINDOWS)], axis=2)
    y = jnp.einsum('bsgc,gco->bsgo', mixed, w_b_group) + b_b_group
    return y.reshape(bsz, seq, D_MODEL) * ls_b


def setup_inputs(seed: int = 0) -> dict:
    key = jax.random.key(seed)
    ks = jax.random.split(key, 32)
    f = jnp.float32
    D = D_MODEL

    def nrm(k, shape, scale):
        return jax.random.normal(k, shape, f) * scale

    def gain(k, shape):
        return 1.0 + 0.05 * jax.random.normal(k, shape, f)

    return {
        'x': nrm(ks[0], (BATCH, SEQ, D), 1.0),
        'c': nrm(ks[1], (BATCH, D), 1.0),
        'w_ada': nrm(ks[2], (D, N_ADA * 3 * D), 0.5 * D ** -0.5),
        'b_ada': nrm(ks[3], (N_ADA * 3 * D,), 0.01),
        'g_ffn1': gain(ks[4], (D,)),
        'w1_in': nrm(ks[5], (D, 2 * D_FF), D ** -0.5),
        'w1_out': nrm(ks[6], (D_FF, D), D_FF ** -0.5),
        'g_mix': gain(ks[7], (D,)),
        'w_in': nrm(ks[8], (D, N_IN_COLS), D ** -0.5),
        'conv_w': nrm(ks[9], (CONV_K, W_CONV), CONV_K ** -0.5),
        'conv_b': nrm(ks[10], (W_CONV,), 0.01),
        'ln_a_g': gain(ks[11], (W_CONV,)),
        'ln_a_b': nrm(ks[12], (W_CONV,), 0.01),
        'w_a_out': nrm(ks[13], (W_CONV, D), W_CONV ** -0.5),
        'b_a_out': nrm(ks[14], (D,), 0.01),
        'w_b_group': nrm(ks[15], (N_POOL_GROUPS, POOL_GROUP_IN, POOL_GROUP_OUT), POOL_GROUP_IN ** -0.5),
        'b_b_group': nrm(ks[16], (N_POOL_GROUPS, POOL_GROUP_OUT), 0.01),
        'ls_b': gain(ks[17], (D,)),
        'w_out': nrm(ks[18], (D, D), D ** -0.5),
        'g_ffn2': gain(ks[19], (D,)),
        'w2_in': nrm(ks[20], (D, 2 * D_FF), D ** -0.5),
        'w2_out': nrm(ks[21], (D_FF, D), D_FF ** -0.5),
        'g_final': gain(ks[22], (D,)),
    }


def reference(x, c, w_ada, b_ada, g_ffn1, w1_in, w1_out, g_mix, w_in, conv_w, conv_b,
              ln_a_g, ln_a_b, w_a_out, b_a_out, w_b_group, b_b_group, ls_b, w_out,
              g_ffn2, w2_in, w2_out, g_final):
    bsz = x.shape[0]
    ada = (jax.nn.silu(c) @ w_ada + b_ada).reshape(bsz, N_ADA, 3, D_MODEL)
    h = x
    for _ in range(DEPTH):
        n = modulate(rms_norm(h, g_ffn1), ada[:, 0, 0], ada[:, 0, 1])
        h = h + 0.5 * ada[:, 0, 2][:, None, :] * swiglu(n, w1_in, w1_out)

        n = modulate(rms_norm(h, g_mix), ada[:, 1, 0], ada[:, 1, 1])
        proj = n @ w_in
        glu_in = proj[..., :2 * W_CONV]
        pool_in = proj[..., 2 * W_CONV:2 * W_CONV + W_POOL]
        gate_a = jax.nn.sigmoid(proj[..., 2 * W_CONV + W_POOL:2 * W_CONV + W_POOL + D_MODEL])
        gate_b = jax.nn.sigmoid(proj[..., 2 * W_CONV + W_POOL + D_MODEL:])
        y_a = conformer_conv_branch(glu_in, conv_w, conv_b, ln_a_g, ln_a_b, w_a_out, b_a_out)
        y_b = pool_branch(pool_in, w_b_group, b_b_group, ls_b)
        mix = (gate_a * y_a + gate_b * y_b) @ w_out
        h = h + ada[:, 1, 2][:, None, :] * mix

        n = modulate(rms_norm(h, g_ffn2), ada[:, 2, 0], ada[:, 2, 1])
        h = h + 0.5 * ada[:, 2, 2][:, None, :] * swiglu(n, w2_in, w2_out)
    return rms_norm(h, g_final)
```

```python
import functools

import jax
import jax.numpy as jnp
from jax.experimental import pallas as pl
from jax.experimental.pallas import tpu as pltpu

EPS = 1e-6
N_ADA = 3
CONV_K = 31
POOL_WINDOWS = (2, 4, 8, 16)

V7X_VMEM_BYTES = 64 * 1024 * 1024
V7X_VMEM_RESERVE_BYTES = 8 * 1024 * 1024
SUBLANES_F32 = 8
SUBLANES_BF16 = 16

CONV_HALO = 32
POOL_HALO = 16

BF16 = jnp.bfloat16
F32 = jnp.float32


def _vmem_limit(pipelined_bytes, scratch_bytes, temp_bytes):
    want = 2 * pipelined_bytes + scratch_bytes + temp_bytes
    return int(min(want, V7X_VMEM_BYTES - V7X_VMEM_RESERVE_BYTES))


def _nbytes(shape, dtype):
    n = 1
    for s in shape:
        n *= s
    return n * jnp.dtype(dtype).itemsize


def _sigmoid(x):
    return 1.0 / (1.0 + jnp.exp(-x))


def _silu(x):
    return x * _sigmoid(x)


def _rms_norm(x, g):
    ms = jnp.mean(x * x, axis=-1, keepdims=True)
    return (x * jax.lax.rsqrt(ms + EPS)) * g


def _dot(a, b):
    return jnp.dot(a, b, preferred_element_type=F32)


def _ada_kernel(c_ref, w_ref, b_ref, o_ref):
    s = _silu(c_ref[...]).astype(BF16)
    o_ref[...] = _dot(s, w_ref[...].astype(BF16)) + b_ref[...]


def _ada_call(c_pad, w_ada, b_ada, *, tn):
    rows, d = c_pad.shape
    n_out = w_ada.shape[1]
    blocks = _nbytes((d, tn), F32) + _nbytes((rows, tn), F32) * 2 + _nbytes((rows, d), F32)
    return pl.pallas_call(
        _ada_kernel,
        grid=(n_out // tn,),
        in_specs=[
            pl.BlockSpec((rows, d), lambda j: (0, 0)),
            pl.BlockSpec((d, tn), lambda j: (0, j)),
            pl.BlockSpec((1, tn), lambda j: (0, j)),
        ],
        out_specs=pl.BlockSpec((rows, tn), lambda j: (0, j)),
        out_shape=jax.ShapeDtypeStruct((rows, n_out), F32),
        compiler_params=pltpu.CompilerParams(
            dimension_semantics=("arbitrary",),
            vmem_limit_bytes=_vmem_limit(blocks, 0, _nbytes((d, tn), BF16)),
        ),
        name="ada_proj",
    )(c_pad, w_ada, b_ada)


def _ffn_kernel(h_ref, ada_ref, g_ref, wh_ref, wu_ref, wo_ref, gf_ref, o_ref, n_ref,
                *, ada_row, final_norm):
    f = pl.program_id(1)

    @pl.when(f == 0)
    def _():
        x = h_ref[...]
        shift = ada_ref[ada_row:ada_row + 1, :]
        scale = ada_ref[ada_row + 1:ada_row + 2, :]
        n = _rms_norm(x, g_ref[...]) * (1.0 + scale) + shift
        n_ref[...] = n.astype(BF16)
        o_ref[...] = jnp.zeros_like(o_ref)

    n = n_ref[...]
    hh = _dot(n, wh_ref[...])
    uu = _dot(n, wu_ref[...])
    act = (_silu(hh) * uu).astype(BF16)
    o_ref[...] += _dot(act, wo_ref[...])

    @pl.when(f == pl.num_programs(1) - 1)
    def _():
        gate = ada_ref[ada_row + 2:ada_row + 3, :]
        y = h_ref[...] + (0.5 * gate) * o_ref[...]
        if final_norm:
            y = _rms_norm(y, gf_ref[...])
        o_ref[...] = y


def _ffn_call(h, ada, g, w_in, w_out, g_final, *, layer, final_norm, seq, tm, tf):
    t, d = h.shape
    d_ff = w_out.shape[0]
    tiles_per_seq = seq // tm
    n_f = d_ff // tf
    kern = functools.partial(_ffn_kernel, ada_row=3 * layer, final_norm=final_norm)
    blocks = (2 * _nbytes((tm, d), F32) + _nbytes((N_ADA * 3, d), F32) + 2 * _nbytes((1, d), F32)
              + 3 * _nbytes((d, tf), BF16))
    scratch = _nbytes((tm, d), BF16)
    temps = 2 * _nbytes((tm, tf), F32) + _nbytes((tm, tf), BF16) + 2 * _nbytes((tm, d), F32)
    return pl.pallas_call(
        kern,
        grid=(t // tm, n_f),
        in_specs=[
            pl.BlockSpec((tm, d), lambda i, f: (i, 0)),
            pl.BlockSpec((None, N_ADA * 3, d), lambda i, f: (i // tiles_per_seq, 0, 0)),
            pl.BlockSpec((1, d), lambda i, f: (0, 0)),
            pl.BlockSpec((d, tf), lambda i, f: (0, f)),
            pl.BlockSpec((d, tf), lambda i, f: (0, f + n_f)),
            pl.BlockSpec((tf, d), lambda i, f: (f, 0)),
            pl.BlockSpec((1, d), lambda i, f: (0, 0)),
        ],
        out_specs=pl.BlockSpec((tm, d), lambda i, f: (i, 0)),
        out_shape=jax.ShapeDtypeStruct((t, d), F32),
        scratch_shapes=[pltpu.VMEM((tm, d), BF16)],
        compiler_params=pltpu.CompilerParams(
            dimension_semantics=("arbitrary", "arbitrary"),
            vmem_limit_bytes=_vmem_limit(blocks, scratch, temps),
        ),
        name=f"ffn{layer}",
    )(h, ada, g, w_in, w_in, w_out, g_final)


def _mix_in_kernel(h_ref, ada_ref, g_ref, wa_ref, wb_ref, oa_ref, op_ref, og_ref, n_ref):
    j = pl.program_id(1)

    @pl.when(j == 0)
    def _():
        shift = ada_ref[3:4, :]
        scale = ada_ref[4:5, :]
        n = _rms_norm(h_ref[...], g_ref[...]) * (1.0 + scale) + shift
        n_ref[...] = n.astype(BF16)

    pa = _dot(n_ref[...], wa_ref[...])

    @pl.when(j == 0)
    def _():
        pb = _dot(n_ref[...], wb_ref[...])
        oa_ref[...] = pa * _sigmoid(pb)

    @pl.when(j == 1)
    def _():
        op_ref[...] = pa

    @pl.when(j >= 2)
    def _():
        og_ref[...] = _sigmoid(pa)


def _mix_in_call(h, ada, g, w_in, *, seq, tm, w_conv, w_pool):
    t, d = h.shape
    tn = w_conv
    assert w_pool == tn and w_in.shape[1] == 2 * w_conv + w_pool + 2 * d and d % tn == 0
    tiles_per_seq = seq // tm
    n_gate_blocks = 2 * d // tn
    n_steps = 2 + n_gate_blocks
    def col_a(j):
        return jnp.where(j == 0, 0, j + 1)
    blocks = (_nbytes((tm, d), F32) + _nbytes((N_ADA * 3, d), F32) + _nbytes((1, d), F32)
              + 2 * _nbytes((d, tn), BF16) + 3 * _nbytes((tm, tn), F32))
    scratch = _nbytes((tm, d), BF16)
    temps = 2 * _nbytes((tm, tn), F32)
    return pl.pallas_call(
        _mix_in_kernel,
        grid=(t // tm, n_steps),
        in_specs=[
            pl.BlockSpec((tm, d), lambda i, j: (i, 0)),
            pl.BlockSpec((None, N_ADA * 3, d), lambda i, j: (i // tiles_per_seq, 0, 0)),
            pl.BlockSpec((1, d), lambda i, j: (0, 0)),
            pl.BlockSpec((d, tn), lambda i, j: (0, col_a(j))),
            pl.BlockSpec((d, tn), lambda i, j: (0, 1)),
        ],
        out_specs=[
            pl.BlockSpec((tm, tn), lambda i, j: (i, 0)),
            pl.BlockSpec((tm, tn), lambda i, j: (i, 0)),
            pl.BlockSpec((tm, tn), lambda i, j: (i, jnp.maximum(j - 2, 0))),
        ],
        out_shape=[
            jax.ShapeDtypeStruct((t, w_conv), F32),
            jax.ShapeDtypeStruct((t, w_pool), F32),
            jax.ShapeDtypeStruct((t, 2 * d), F32),
        ],
        scratch_shapes=[pltpu.VMEM((tm, d), BF16)],
        compiler_params=pltpu.CompilerParams(
            dimension_semantics=("arbitrary", "arbitrary"),
            vmem_limit_bytes=_vmem_limit(blocks, scratch, temps),
        ),
        name="mix_in",
    )(h, ada, g, w_in, w_in)


CONV_ROWS = 32
CONV_LANES = 512


def _mix_body_kernel(a_ref, ah_ref, p_ref, ph_ref, gate_ref, h_ref, ada_ref,
                     cw_ref, cb_ref, lng_ref, lnb_ref, wa_ref, ba_ref,
                     wb_ref, bb_ref, ls_ref, wo_ref, o_ref,
                     aext_ref, pext_ref, conv_ref, z_ref, *, tiles_per_seq):
    i = pl.program_id(0)
    tm, d = h_ref.shape
    w_conv = a_ref.shape[1]
    seq_tile = i % tiles_per_seq
    seq_start = seq_tile == 0

    aext_ref[0:CONV_HALO, :] = jnp.where(seq_start, 0.0, ah_ref[...])
    aext_ref[CONV_HALO:, :] = a_ref[...]
    pext_ref[0:POOL_HALO, :] = jnp.where(seq_start, 0.0, ph_ref[...])
    pext_ref[POOL_HALO:, :] = p_ref[...]

    base = CONV_HALO - (CONV_K - 1)
    for c0 in range(0, w_conv, CONV_LANES):
        cs = slice(c0, c0 + CONV_LANES)
        for r0 in range(0, tm, CONV_ROWS):
            acc = None
            for k in range(CONV_K):
                lo = r0 + base + k
                term = aext_ref[lo:lo + CONV_ROWS, cs] * cw_ref[k:k + 1, cs]
                acc = term if acc is None else acc + term
            conv_ref[r0:r0 + CONV_ROWS, cs] = acc + cb_ref[:, cs]

    cv = conv_ref[...]
    mu = jnp.mean(cv, axis=-1, keepdims=True)
    dv = cv - mu
    var = jnp.mean(dv * dv, axis=-1, keepdims=True)
    y = (dv * jax.lax.rsqrt(var + EPS)) * lng_ref[...] + lnb_ref[...]
    ya = _dot(_silu(y).astype(BF16), wa_ref[...]) + ba_ref[...]

    pos = seq_tile * tm + jax.lax.broadcasted_iota(jnp.int32, (tm, 1), 0)
    gin = p_ref.shape[1] // len(POOL_WINDOWS)
    gout = d // len(POOL_WINDOWS)
    for g, w in enumerate(POOL_WINDOWS):
        cin = slice(g * gin, (g + 1) * gin)
        cout = slice(g * gout, (g + 1) * gout)
        v = pext_ref[POOL_HALO:POOL_HALO + tm, cin]
        acc = v
        for j in range(1, w):
            acc = acc + pext_ref[POOL_HALO - j:POOL_HALO - j + tm, cin]
        count = jnp.minimum(pos + 1, w).astype(F32)
        mixed = acc / count - v
        yb = (_dot(mixed.astype(BF16), wb_ref[g]) + bb_ref[g:g + 1, :]) * ls_ref[:, cout]
        z = gate_ref[:, cout] * ya[:, cout] + gate_ref[:, d + g * gout:d + (g + 1) * gout] * yb
        z_ref[:, cout] = z.astype(BF16)

    mix = _dot(z_ref[...], wo_ref[...])
    o_ref[...] = h_ref[...] + ada_ref[5:6, :] * mix


def _mix_body_call(a, p, gates, h, ada, conv_w, conv_b, ln_g, ln_b, w_a_out, b_a_out,
                   w_b_group, b_b_group, ls_b, w_out, *, seq, tm):
    t, d = h.shape
    w_conv = a.shape[1]
    w_pool = p.shape[1]
    n_groups, gin, gout = w_b_group.shape
    tiles_per_seq = seq // tm
    a_halo_blocks = tm // CONV_HALO
    p_halo_blocks = tm // POOL_HALO
    const = lambda i: (0, 0)
    kern = functools.partial(_mix_body_kernel, tiles_per_seq=tiles_per_seq)
    blocks = (_nbytes((tm + CONV_HALO, w_conv), F32) + _nbytes((tm + POOL_HALO, w_pool), F32)
              + _nbytes((tm, 2 * d), F32) + 2 * _nbytes((tm, d), F32)
              + _nbytes((w_conv, d), BF16) + _nbytes((n_groups, gin, gout), BF16)
              + _nbytes((d, d), BF16) + _nbytes((64, d), F32))
    scratch = (_nbytes((tm + CONV_HALO, w_conv), F32) + _nbytes((tm + POOL_HALO, w_pool), F32)
               + _nbytes((tm, w_conv), F32) + _nbytes((tm, d), BF16))
    temps = 4 * _nbytes((tm, d), F32)
    return pl.pallas_call(
        kern,
        grid=(t // tm,),
        in_specs=[
            pl.BlockSpec((tm, w_conv), lambda i: (i, 0)),
            pl.BlockSpec((CONV_HALO, w_conv), lambda i: (jnp.maximum(i * a_halo_blocks - 1, 0), 0)),
            pl.BlockSpec((tm, w_pool), lambda i: (i, 0)),
            pl.BlockSpec((POOL_HALO, w_pool), lambda i: (jnp.maximum(i * p_halo_blocks - 1, 0), 0)),
            pl.BlockSpec((tm, 2 * d), lambda i: (i, 0)),
            pl.BlockSpec((tm, d), lambda i: (i, 0)),
            pl.BlockSpec((None, N_ADA * 3, d), lambda i: (i // tiles_per_seq, 0, 0)),
            pl.BlockSpec((CONV_K, w_conv), const),
            pl.BlockSpec((1, w_conv), const),
            pl.BlockSpec((1, w_conv), const),
            pl.BlockSpec((1, w_conv), const),
            pl.BlockSpec((w_conv, d), const),
            pl.BlockSpec((1, d), const),
            pl.BlockSpec((n_groups, gin, gout), lambda i: (0, 0, 0)),
            pl.BlockSpec((n_groups, gout), const),
            pl.BlockSpec((1, d), const),
            pl.BlockSpec((d, d), const),
        ],
        out_specs=pl.BlockSpec((tm, d), lambda i: (i, 0)),
        out_shape=jax.ShapeDtypeStruct((t, d), F32),
        scratch_shapes=[
            pltpu.VMEM((tm + CONV_HALO, w_conv), F32),
            pltpu.VMEM((tm + POOL_HALO, w_pool), F32),
            pltpu.VMEM((tm, w_conv), F32),
            pltpu.VMEM((tm, d), BF16),
        ],
        compiler_params=pltpu.CompilerParams(
            dimension_semantics=("arbitrary",),
            vmem_limit_bytes=_vmem_limit(blocks, scratch, temps),
        ),
        name="mix_body",
    )(a, a, p, p, gates, h, ada, conv_w, conv_b, ln_g, ln_b, w_a_out, b_a_out,
      w_b_group, b_b_group, ls_b, w_out)


FFN_TM = 512
FFN_TF = 512
MIX_IN_TM = 512
MIX_BODY_TM = 256
ADA_TN = 2048


def kernel(x, c, w_ada, b_ada, g_ffn1, w1_in, w1_out, g_mix, w_in, conv_w, conv_b,
           ln_a_g, ln_a_b, w_a_out, b_a_out, w_b_group, b_b_group, ls_b, w_out,
           g_ffn2, w2_in, w2_out, g_final):
    bsz, seq, d = x.shape
    w_conv = conv_w.shape[1]
    w_pool = w_b_group.shape[0] * w_b_group.shape[1]
    row = lambda v: v.reshape(1, -1)

    c_pad = jnp.pad(c, ((0, SUBLANES_BF16 - bsz), (0, 0)))
    ada = _ada_call(c_pad, w_ada, row(b_ada), tn=ADA_TN)[:bsz]
    ada = ada.reshape(bsz, N_ADA * 3, d)

    h = x.reshape(bsz * seq, d)
    h = _ffn_call(h, ada, row(g_ffn1), w1_in.astype(BF16), w1_out.astype(BF16), row(g_final),
                  layer=0, final_norm=False, seq=seq, tm=FFN_TM, tf=FFN_TF)

    a, p, gates = _mix_in_call(h, ada, row(g_mix), w_in.astype(BF16),
                               seq=seq, tm=MIX_IN_TM, w_conv=w_conv, w_pool=w_pool)
    h = _mix_body_call(a, p, gates, h, ada, conv_w, row(conv_b), row(ln_a_g), row(ln_a_b),
                       w_a_out.astype(BF16), row(b_a_out), w_b_group.astype(BF16), b_b_group,
                       row(ls_b), w_out.astype(BF16), seq=seq, tm=MIX_BODY_TM)

    h = _ffn_call(h, ada, row(g_ffn2), w2_in.astype(BF16), w2_out.astype(BF16), row(g_final),
                  layer=2, final_norm=True, seq=seq, tm=FFN_TM, tf=FFN_TF)
    return h.reshape(bsz, seq, d)
```

```python
import functools

import jax
import jax.numpy as jnp
from jax.experimental import pallas as pl
from jax.experimental.pallas import tpu as pltpu

EPS = 1e-6
N_ADA = 3
CONV_K = 31
POOL_WINDOWS = (2, 4, 8, 16)

V7X_VMEM_BYTES = 64 * 1024 * 1024
V7X_VMEM_RESERVE_BYTES = 8 * 1024 * 1024
LANES = 128
SUBLANES_F32 = 8
SUBLANES_BF16 = 16

CONV_HALO = 32
POOL_HALO = 16

BF16 = jnp.bfloat16
F32 = jnp.float32


def _vmem_limit(pipelined_bytes, scratch_bytes, temp_bytes):
    want = 2 * pipelined_bytes + scratch_bytes + temp_bytes
    return int(min(want, V7X_VMEM_BYTES - V7X_VMEM_RESERVE_BYTES))


def _nbytes(shape, dtype):
    n = 1
    for s in shape:
        n *= s
    return n * jnp.dtype(dtype).itemsize


def _sigmoid(x):
    return 1.0 / (1.0 + jnp.exp(-x))


def _silu(x):
    return x * _sigmoid(x)


def _rms_norm(x, g):
    ms = jnp.mean(x * x, axis=-1, keepdims=True)
    return (x * jax.lax.rsqrt(ms + EPS)) * g


def _dot(a, b):
    return jnp.dot(a, b, preferred_element_type=F32)


def _ada_kernel(c_ref, w_ref, b_ref, o_ref):
    s = _silu(c_ref[...]).astype(BF16)
    o_ref[...] = _dot(s, w_ref[...].astype(BF16)) + b_ref[...]


def _ada_call(c_pad, w_ada, b_ada, *, tn):
    rows, d = c_pad.shape
    n_out = w_ada.shape[1]
    blocks = _nbytes((d, tn), F32) + _nbytes((rows, tn), F32) * 2 + _nbytes((rows, d), F32)
    return pl.pallas_call(
        _ada_kernel,
        grid=(n_out // tn,),
        in_specs=[
            pl.BlockSpec((rows, d), lambda j: (0, 0)),
            pl.BlockSpec((d, tn), lambda j: (0, j)),
            pl.BlockSpec((1, tn), lambda j: (0, j)),
        ],
        out_specs=pl.BlockSpec((rows, tn), lambda j: (0, j)),
        out_shape=jax.ShapeDtypeStruct((rows, n_out), F32),
        compiler_params=pltpu.CompilerParams(
            dimension_semantics=("arbitrary",),
            vmem_limit_bytes=_vmem_limit(blocks, 0, _nbytes((d, tn), BF16)),
        ),
        name="ada_proj",
    )(c_pad, w_ada, b_ada)


FFN_ROW_CHUNKS = 4
FFN_OUT_LANES = 512


def _ffn_kernel(h_hbm, ada_ref, g_ref, wh_ref, wu_ref, wo_ref, g2_ref, o_ref, *rest,
                ada_row, epilogue):
    if epilogue == "next_norm":
        nnext_ref, n_ref, sem = rest
    else:
        n_ref, sem = rest
    i = pl.program_id(0)
    f = pl.program_id(1)
    tm, d = o_ref.shape
    rc = tm // FFN_ROW_CHUNKS

    @pl.when(f == 0)
    def _():
        def copy(c):
            return pltpu.make_async_copy(h_hbm.at[pl.ds(i * tm + c * rc, rc), :],
                                         o_ref.at[pl.ds(c * rc, rc), :], sem.at[c])
        for c in range(FFN_ROW_CHUNKS):
            copy(c).start()
        shift = ada_ref[ada_row:ada_row + 1, :]
        scale = ada_ref[ada_row + 1:ada_row + 2, :]
        for c in range(FFN_ROW_CHUNKS):
            copy(c).wait()
            x = o_ref[c * rc:(c + 1) * rc, :]
            n = _rms_norm(x, g_ref[...]) * (1.0 + scale) + shift
            n_ref[c * rc:(c + 1) * rc, :] = n.astype(BF16)

    n = n_ref[...]
    hh = _dot(n, wh_ref[...].astype(BF16))
    uu = _dot(n, wu_ref[...].astype(BF16))
    act = (_silu(hh) * uu).astype(BF16)
    half_gate = 0.5 * ada_ref[ada_row + 2:ada_row + 3, :]
    for c0 in range(0, d, FFN_OUT_LANES):
        cs = slice(c0, c0 + FFN_OUT_LANES)
        o_ref[:, cs] += half_gate[:, cs] * _dot(act, wo_ref[:, cs].astype(BF16))

    @pl.when(f == pl.num_programs(1) - 1)
    def _():
        y = o_ref[...]
        if epilogue == "next_norm":
            shift = ada_ref[ada_row + 3:ada_row + 4, :]
            scale = ada_ref[ada_row + 4:ada_row + 5, :]
            nn = _rms_norm(y, g2_ref[...]) * (1.0 + scale) + shift
            nnext_ref[...] = nn.astype(BF16)
        else:
            o_ref[...] = _rms_norm(y, g2_ref[...])


def _ffn_call(h, ada, g, w_in, w_out, g2, *, layer, epilogue, seq, tm, tf):
    t, d = h.shape
    d_ff = w_out.shape[0]
    tiles_per_seq = seq // tm
    n_f = d_ff // tf
    assert tm % (FFN_ROW_CHUNKS * SUBLANES_BF16) == 0 and d % FFN_OUT_LANES == 0
    kern = functools.partial(_ffn_kernel, ada_row=3 * layer, epilogue=epilogue)
    out_specs = [pl.BlockSpec((tm, d), lambda i, f: (i, 0))]
    out_shape = [jax.ShapeDtypeStruct((t, d), F32)]
    blocks = (_nbytes((tm, d), F32) + _nbytes((N_ADA * 3, d), F32) + 2 * _nbytes((1, d), F32)
              + 3 * _nbytes((d, tf), F32))
    if epilogue == "next_norm":
        out_specs.append(pl.BlockSpec((tm, d), lambda i, f: (i, 0)))
        out_shape.append(jax.ShapeDtypeStruct((t, d), BF16))
        blocks += _nbytes((tm, d), BF16)
    scratch = _nbytes((tm, d), BF16)
    temps = (3 * _nbytes((d, tf), BF16) + 3 * _nbytes((tm, tf), F32) + _nbytes((tm, tf), BF16)
             + 2 * _nbytes((tm, FFN_OUT_LANES), F32) + 2 * _nbytes((tm // FFN_ROW_CHUNKS, d), F32))
    return pl.pallas_call(
        kern,
        grid=(t // tm, n_f),
        in_specs=[
            pl.BlockSpec(memory_space=pl.ANY),
            pl.BlockSpec((None, N_ADA * 3, d), lambda i, f: (i // tiles_per_seq, 0, 0)),
            pl.BlockSpec((1, d), lambda i, f: (0, 0)),
            pl.BlockSpec((d, tf), lambda i, f: (0, f)),
            pl.BlockSpec((d, tf), lambda i, f: (0, f + n_f)),
            pl.BlockSpec((tf, d), lambda i, f: (f, 0)),
            pl.BlockSpec((1, d), lambda i, f: (0, 0)),
        ],
        out_specs=out_specs,
        out_shape=out_shape,
        scratch_shapes=[pltpu.VMEM((tm, d), BF16), pltpu.SemaphoreType.DMA((FFN_ROW_CHUNKS,))],
        compiler_params=pltpu.CompilerParams(
            dimension_semantics=("arbitrary", "arbitrary"),
            vmem_limit_bytes=_vmem_limit(blocks, scratch, temps),
        ),
        name=f"ffn{layer}",
    )(h, ada, g, w_in, w_in, w_out, g2)


def _mix_in_kernel(n_ref, w_ref, q_ref, wbf_ref, *, raw_blocks):
    j = pl.program_id(0)

    @pl.when(pl.program_id(1) == 0)
    def _():
        wbf_ref[...] = w_ref[...].astype(BF16)

    pa = _dot(n_ref[...], wbf_ref[...])
    raw = functools.reduce(jnp.logical_or, [j == b for b in raw_blocks])
    q_ref[...] = jnp.where(raw, pa, _sigmoid(pa))


def _mix_in_call(n, w_in, *, tm, tn, raw_blocks):
    t, d = n.shape
    n_cols = w_in.shape[1]
    kern = functools.partial(_mix_in_kernel, raw_blocks=raw_blocks)
    blocks = _nbytes((tm, d), BF16) + _nbytes((d, tn), F32) + _nbytes((tm, tn), F32)
    scratch = _nbytes((d, tn), BF16)
    temps = 3 * _nbytes((tm, tn), F32)
    return pl.pallas_call(
        kern,
        grid=(n_cols // tn, t // tm),
        in_specs=[
            pl.BlockSpec((tm, d), lambda j, i: (i, 0)),
            pl.BlockSpec((d, tn), lambda j, i: (0, j)),
        ],
        out_specs=pl.BlockSpec((tm, tn), lambda j, i: (i, j)),
        out_shape=jax.ShapeDtypeStruct((t, n_cols), F32),
        scratch_shapes=[pltpu.VMEM((d, tn), BF16)],
        compiler_params=pltpu.CompilerParams(
            dimension_semantics=("arbitrary", "arbitrary"),
            vmem_limit_bytes=_vmem_limit(blocks, scratch, temps),
        ),
        name="mix_in",
    )(n, w_in)


CONV_STEPS = 16


def _to_time_major(dst_ref, row0, src, n_rows):
    for cb in range(src.shape[1] // LANES):
        dst_ref[pl.ds(row0 * SUBLANES_F32 + cb, n_rows, stride=SUBLANES_F32), :] = (
            src[:, cb * LANES:(cb + 1) * LANES])


def _from_time_major(src_ref, n_rows, cb_lo, cb_hi):
    return jnp.concatenate(
        [src_ref[pl.ds(cb, n_rows, stride=SUBLANES_F32), :] for cb in range(cb_lo, cb_hi)],
        axis=-1)


def _mix_body_kernel(av_ref, as_ref, avh_ref, ash_ref, p_ref, ph_ref,
                     ga0_ref, ga1_ref, gb0_ref, gb1_ref, h_ref, ada_ref,
                     cw_ref, cb_ref, lng_ref, lnb_ref, wa_ref, ba_ref,
                     wb_ref, bb_ref, ls_ref, wo_ref, o_ref,
                     a2d_ref, c2d_ref, p2d_ref, m2d_ref, z_ref, *, tiles_per_seq):
    i = pl.program_id(0)
    tm, d = h_ref.shape
    w_conv = av_ref.shape[1]
    n_cb = w_conv // LANES
    seq_tile = i % tiles_per_seq
    seq_start = seq_tile == 0

    a_hist = jnp.where(seq_start, 0.0, avh_ref[...] * ash_ref[...])
    _to_time_major(a2d_ref, 0, a_hist, CONV_HALO)
    _to_time_major(a2d_ref, CONV_HALO, av_ref[...] * as_ref[...], tm)
    p_hist = jnp.where(seq_start, 0.0, ph_ref[...])
    _to_time_major(p2d_ref, 0, p_hist, POOL_HALO)
    _to_time_major(p2d_ref, POOL_HALO, p_ref[...], tm)

    base = CONV_HALO - (CONV_K - 1)

    def conv_group(gi, carry):
        t0 = gi * CONV_STEPS
        accs = [None] * CONV_STEPS
        for k in range(CONV_K):
            wk = cw_ref[k * SUBLANES_F32:(k + 1) * SUBLANES_F32, :]
            for s in range(CONV_STEPS):
                row = pl.multiple_of((t0 + s + base + k) * SUBLANES_F32, SUBLANES_F32)
                term = a2d_ref[pl.ds(row, SUBLANES_F32), :] * wk
                accs[s] = term if accs[s] is None else accs[s] + term
        for s in range(CONV_STEPS):
            row = pl.multiple_of((t0 + s) * SUBLANES_F32, SUBLANES_F32)
            c2d_ref[pl.ds(row, SUBLANES_F32), :] = accs[s] + cb_ref[...]
        return carry

    jax.lax.fori_loop(0, tm // CONV_STEPS, conv_group, 0)

    cv = _from_time_major(c2d_ref, tm, 0, n_cb)
    mu = jnp.mean(cv, axis=-1, keepdims=True)
    dv = cv - mu
    var = jnp.mean(dv * dv, axis=-1, keepdims=True)
    y = (dv * jax.lax.rsqrt(var + EPS)) * lng_ref[...] + lnb_ref[...]
    ya = _dot(_silu(y).astype(BF16), wa_ref[...]) + ba_ref[...]

    def lag(j):
        return p2d_ref[pl.ds((POOL_HALO - j) * SUBLANES_F32, tm * SUBLANES_F32), :]

    v = lag(0)
    s2 = v + lag(1)
    s4 = s2 + (lag(2) + lag(3))
    s8 = s4 + ((lag(4) + lag(5)) + (lag(6) + lag(7)))
    s16 = s8 + (((lag(8) + lag(9)) + (lag(10) + lag(11)))
                + ((lag(12) + lag(13)) + (lag(14) + lag(15))))
    row = jax.lax.broadcasted_iota(jnp.int32, (tm * SUBLANES_F32, LANES), 0)
    group = jax.lax.shift_right_logical(jnp.bitwise_and(row, SUBLANES_F32 - 1), 1)
    pos = seq_tile * tm + jax.lax.shift_right_logical(row, 3)
    window = jnp.left_shift(2, group)
    count = jnp.minimum(pos + 1, window).astype(F32)
    wsum = jnp.where(group == 0, s2, jnp.where(group == 1, s4, jnp.where(group == 2, s8, s16)))
    m2d_ref[...] = wsum / count - v

    n_groups = len(POOL_WINDOWS)
    gout = d // n_groups
    cb_per_group = p_ref.shape[1] // LANES // n_groups
    gate_a = (ga0_ref, ga1_ref)
    gate_b = (gb0_ref, gb1_ref)
    half = d // 2
    for g in range(n_groups):
        cout = slice(g * gout, (g + 1) * gout)
        gsrc = slice((g * gout) % half, (g * gout) % half + gout)
        mixed = _from_time_major(m2d_ref, tm, g * cb_per_group, (g + 1) * cb_per_group)
        yb = (_dot(mixed.astype(BF16), wb_ref[g]) + bb_ref[g:g + 1, :]) * ls_ref[:, cout]
        ga = gate_a[(g * gout) // half][:, gsrc]
        gb = gate_b[(g * gout) // half][:, gsrc]
        z_ref[:, cout] = (ga * ya[:, cout] + gb * yb).astype(BF16)

    mix = _dot(z_ref[...], wo_ref[...])
    o_ref[...] = h_ref[...] + ada_ref[5:6, :] * mix


def _mix_body_call(q, h, ada, conv_w, conv_b, ln_g, ln_b, w_a_out, b_a_out,
                   w_b_group, b_b_group, ls_b, w_out, *, seq, tm, w_conv, w_pool):
    t, d = h.shape
    n_groups, gin, gout = w_b_group.shape
    assert w_conv == w_pool == d // 2 and q.shape[1] == 2 * w_conv + w_pool + 2 * d
    assert w_conv // LANES == SUBLANES_F32
    tiles_per_seq = seq // tm
    a_halo_blocks = tm // CONV_HALO
    p_halo_blocks = tm // POOL_HALO
    const = lambda i: (0, 0)
    resident = pl.Buffered(1)
    kern = functools.partial(_mix_body_kernel, tiles_per_seq=tiles_per_seq)
    wcol = w_conv

    def qblock(col):
        return pl.BlockSpec((tm, wcol), lambda i: (i, col))

    blocks = 7 * _nbytes((tm, wcol), F32) + 2 * _nbytes((tm, d), F32) + _nbytes((128, d), F32)
    weights = (_nbytes((w_conv, d), BF16) + _nbytes((n_groups, gin, gout), BF16)
               + _nbytes((d, d), BF16))
    scratch = (_nbytes((tm + CONV_HALO, w_conv), F32) + _nbytes((tm + POOL_HALO, w_pool), F32)
               + 2 * _nbytes((tm, w_conv), F32) + _nbytes((tm, d), BF16))
    temps = 4 * _nbytes((tm, d), F32)
    return pl.pallas_call(
        kern,
        grid=(t // tm,),
        in_specs=[
            qblock(0),
            qblock(1),
            pl.BlockSpec((CONV_HALO, wcol), lambda i: (jnp.maximum(i * a_halo_blocks - 1, 0), 0)),
            pl.BlockSpec((CONV_HALO, wcol), lambda i: (jnp.maximum(i * a_halo_blocks - 1, 0), 1)),
            qblock(2),
            pl.BlockSpec((POOL_HALO, wcol), lambda i: (jnp.maximum(i * p_halo_blocks - 1, 0), 2)),
            qblock(3),
            qblock(4),
            qblock(5),
            qblock(6),
            pl.BlockSpec((tm, d), lambda i: (i, 0)),
            pl.BlockSpec((None, N_ADA * 3, d), lambda i: (i // tiles_per_seq, 0, 0)),
            pl.BlockSpec((CONV_K * SUBLANES_F32, LANES), const),
            pl.BlockSpec((SUBLANES_F32, LANES), const),
            pl.BlockSpec((1, w_conv), const),
            pl.BlockSpec((1, w_conv), const),
            pl.BlockSpec((w_conv, d), const, pipeline_mode=resident),
            pl.BlockSpec((1, d), const),
            pl.BlockSpec((n_groups, gin, gout), lambda i: (0, 0, 0), pipeline_mode=resident),
            pl.BlockSpec((n_groups, gout), const),
            pl.BlockSpec((1, d), const),
            pl.BlockSpec((d, d), const, pipeline_mode=resident),
        ],
        out_specs=pl.BlockSpec((tm, d), lambda i: (i, 0)),
        out_shape=jax.ShapeDtypeStruct((t, d), F32),
        scratch_shapes=[
            pltpu.VMEM(((tm + CONV_HALO) * SUBLANES_F32, LANES), F32),
            pltpu.VMEM((tm * SUBLANES_F32, LANES), F32),
            pltpu.VMEM(((tm + POOL_HALO) * SUBLANES_F32, LANES), F32),
            pltpu.VMEM((tm * SUBLANES_F32, LANES), F32),
            pltpu.VMEM((tm, d), BF16),
        ],
        compiler_params=pltpu.CompilerParams(
            dimension_semantics=("arbitrary",),
            vmem_limit_bytes=_vmem_limit(blocks, weights + scratch, temps),
        ),
        name="mix_body",
    )(q, q, q, q, q, q, q, q, q, q, h, ada,
      conv_w.reshape(CONV_K * SUBLANES_F32, LANES), conv_b.reshape(SUBLANES_F32, LANES),
      ln_g, ln_b, w_a_out, b_a_out, w_b_group, b_b_group, ls_b, w_out)


FFN_TM = 1024
FFN_TF = 256
MIX_IN_TM = 1024
MIX_IN_TN = 1024
MIX_BODY_TM = 256
ADA_TN = 2048


def kernel(x, c, w_ada, b_ada, g_ffn1, w1_in, w1_out, g_mix, w_in, conv_w, conv_b,
           ln_a_g, ln_a_b, w_a_out, b_a_out, w_b_group, b_b_group, ls_b, w_out,
           g_ffn2, w2_in, w2_out, g_final):
    bsz, seq, d = x.shape
    w_conv = conv_w.shape[1]
    w_pool = w_b_group.shape[0] * w_b_group.shape[1]
    row = lambda v: v.reshape(1, -1)

    c_pad = jnp.pad(c, ((0, SUBLANES_BF16 - bsz), (0, 0)))
    ada = _ada_call(c_pad, w_ada, row(b_ada), tn=ADA_TN)[:bsz]
    ada = ada.reshape(bsz, N_ADA * 3, d)

    h = x.reshape(bsz * seq, d)
    h, n_mix = _ffn_call(h, ada, row(g_ffn1), w1_in, w1_out, row(g_mix),
                         layer=0, epilogue="next_norm", seq=seq, tm=FFN_TM, tf=FFN_TF)

    raw_blocks = (0, (2 * w_conv) // MIX_IN_TN)
    q = _mix_in_call(n_mix, w_in, tm=MIX_IN_TM, tn=MIX_IN_TN, raw_blocks=raw_blocks)
    h = _mix_body_call(q, h, ada, conv_w, conv_b, row(ln_a_g), row(ln_a_b),
                       w_a_out.astype(BF16), row(b_a_out), w_b_group.astype(BF16), b_b_group,
                       row(ls_b), w_out.astype(BF16), seq=seq, tm=MIX_BODY_TM,
                       w_conv=w_conv, w_pool=w_pool)

    (h,) = _ffn_call(h, ada, row(g_ffn2), w2_in, w2_out, row(g_final),
                     layer=2, epilogue="final_norm", seq=seq, tm=FFN_TM, tf=FFN_TF)
    return h.reshape(bsz, seq, d)
```

```python
import functools

import jax
import jax.numpy as jnp
from jax.experimental import pallas as pl
from jax.experimental.pallas import tpu as pltpu

EPS = 1e-6
N_ADA = 3
CONV_K = 31
POOL_WINDOWS = (2, 4, 8, 16)

V7X_VMEM_BYTES = 64 * 1024 * 1024
V7X_VMEM_RESERVE_BYTES = 6 * 1024 * 1024
SPILL_ALLOWANCE_BYTES = 6 * 1024 * 1024
LANES = 128
SUBLANES_F32 = 8
SUBLANES_BF16 = 16

CONV_HALO = 32
POOL_HALO = 16

BF16 = jnp.bfloat16
F32 = jnp.float32


def _vmem_limit(pipelined_bytes, scratch_bytes, temp_bytes):
    want = 2 * pipelined_bytes + scratch_bytes + temp_bytes
    return int(min(want, V7X_VMEM_BYTES - V7X_VMEM_RESERVE_BYTES))


def _nbytes(shape, dtype):
    n = 1
    for s in shape:
        n *= s
    return n * jnp.dtype(dtype).itemsize


def _sigmoid(x):
    return 1.0 / (1.0 + jnp.exp(-x))


def _silu(x):
    return x * _sigmoid(x)


def _rms_norm(x, g):
    ms = jnp.mean(x * x, axis=-1, keepdims=True)
    return (x * jax.lax.rsqrt(ms + EPS)) * g


def _dot(a, b):
    return jnp.dot(a, b, preferred_element_type=F32)


def _ada_kernel(c_ref, w_ref, b_ref, o_ref):
    s = _silu(c_ref[...]).astype(BF16)
    o_ref[...] = _dot(s, w_ref[...].astype(BF16)) + b_ref[...]


def _ada_call(c_pad, w_ada, b_ada, *, tn):
    rows, d = c_pad.shape
    n_out = w_ada.shape[1]
    blocks = _nbytes((d, tn), F32) + _nbytes((rows, tn), F32) * 2 + _nbytes((rows, d), F32)
    return pl.pallas_call(
        _ada_kernel,
        grid=(n_out // tn,),
        in_specs=[
            pl.BlockSpec((rows, d), lambda j: (0, 0)),
            pl.BlockSpec((d, tn), lambda j: (0, j)),
            pl.BlockSpec((1, tn), lambda j: (0, j)),
        ],
        out_specs=pl.BlockSpec((rows, tn), lambda j: (0, j)),
        out_shape=jax.ShapeDtypeStruct((rows, n_out), F32),
        compiler_params=pltpu.CompilerParams(
            dimension_semantics=("arbitrary",),
            vmem_limit_bytes=_vmem_limit(blocks, 0, _nbytes((d, tn), BF16)),
        ),
        name="ada_proj",
    )(c_pad, w_ada, b_ada)


FFN_ROW_CHUNKS = 4
FFN_OUT_LANES = 512


def _ffn_kernel(h_hbm, ada_ref, g_ref, wh_ref, wu_ref, wo_ref, g2_ref, o_ref, *rest,
                ada_row, epilogue):
    if epilogue == "next_norm":
        nnext_ref, n_ref, x_ref, sem = rest
    else:
        n_ref, x_ref, sem = rest
    i = pl.program_id(0)
    f = pl.program_id(1)
    last = pl.num_programs(1) - 1
    tm, d = o_ref.shape
    rc = tm // FFN_ROW_CHUNKS
    chunks = [slice(c * rc, (c + 1) * rc) for c in range(FFN_ROW_CHUNKS)]

    def fetch(tile):
        return pltpu.make_async_copy(h_hbm.at[pl.ds(tile * tm, tm), :], x_ref, sem.at[0])

    def weights():
        return (wh_ref[...].astype(BF16), wu_ref[...].astype(BF16), wo_ref[...].astype(BF16))

    def normed(rows):
        shift = ada_ref[ada_row:ada_row + 1, :]
        scale = ada_ref[ada_row + 1:ada_row + 2, :]
        n = (_rms_norm(x_ref[rows, :], g_ref[...]) * (1.0 + scale) + shift).astype(BF16)
        n_ref[rows, :] = n
        return n

    def branch(rows, n, wts, base_ref):
        whb, wub, wob = wts
        act = (_silu(_dot(n, whb)) * _dot(n, wub)).astype(BF16)
        half_gate = 0.5 * ada_ref[ada_row + 2:ada_row + 3, :]
        for c0 in range(0, d, FFN_OUT_LANES):
            cs = slice(c0, c0 + FFN_OUT_LANES)
            o_ref[rows, cs] = base_ref[rows, cs] + half_gate[:, cs] * _dot(act, wob[:, cs])

    def finish(rows):
        y = o_ref[rows, :]
        if epilogue == "next_norm":
            shift = ada_ref[ada_row + 3:ada_row + 4, :]
            scale = ada_ref[ada_row + 4:ada_row + 5, :]
            nn = _rms_norm(y, g2_ref[...]) * (1.0 + scale) + shift
            nnext_ref[rows, :] = nn.astype(BF16)
        else:
            o_ref[rows, :] = _rms_norm(y, g2_ref[...])

    @pl.when(f == 0)
    def _():
        @pl.when(i == 0)
        def _():
            fetch(0).start()
        fetch(i).wait()
        wts = weights()
        n = normed(chunks[0])
        for c in range(FFN_ROW_CHUNKS):
            branch(chunks[c], n, wts, x_ref)
            if c + 1 < FFN_ROW_CHUNKS:
                n = normed(chunks[c + 1])

    @pl.when(jnp.logical_and(f == 1, i + 1 < pl.num_programs(0)))
    def _():
        fetch(i + 1).start()

    @pl.when(jnp.logical_and(f > 0, f < last))
    def _():
        branch(slice(None), n_ref[...], weights(), o_ref)

    @pl.when(f == last)
    def _():
        wts = weights()
        for rows in chunks:
            branch(rows, n_ref[rows, :], wts, o_ref)
            finish(rows)


def _ffn_call(h, ada, g, w_in, w_out, g2, *, layer, epilogue, seq, tm, tf):
    t, d = h.shape
    d_ff = w_out.shape[0]
    tiles_per_seq = seq // tm
    n_f = d_ff // tf
    assert tm % (FFN_ROW_CHUNKS * SUBLANES_BF16) == 0 and d % FFN_OUT_LANES == 0 and n_f >= 3
    kern = functools.partial(_ffn_kernel, ada_row=3 * layer, epilogue=epilogue)
    out_specs = [pl.BlockSpec((tm, d), lambda i, f: (i, 0))]
    out_shape = [jax.ShapeDtypeStruct((t, d), F32)]
    blocks = (_nbytes((tm, d), F32) + _nbytes((N_ADA * 3, d), F32) + 2 * _nbytes((1, d), F32)
              + 3 * _nbytes((d, tf), F32))
    if epilogue == "next_norm":
        out_specs.append(pl.BlockSpec((tm, d), lambda i, f: (i, 0)))
        out_shape.append(jax.ShapeDtypeStruct((t, d), BF16))
        blocks += _nbytes((tm, d), BF16)
    scratch = _nbytes((tm, d), BF16) + _nbytes((tm, d), F32)
    return pl.pallas_call(
        kern,
        grid=(t // tm, n_f),
        in_specs=[
            pl.BlockSpec(memory_space=pl.ANY),
            pl.BlockSpec((None, N_ADA * 3, d), lambda i, f: (i // tiles_per_seq, 0, 0)),
            pl.BlockSpec((1, d), lambda i, f: (0, 0)),
            pl.BlockSpec((d, tf), lambda i, f: (0, f)),
            pl.BlockSpec((d, tf), lambda i, f: (0, f + n_f)),
            pl.BlockSpec((tf, d), lambda i, f: (f, 0)),
            pl.BlockSpec((1, d), lambda i, f: (0, 0)),
        ],
        out_specs=out_specs,
        out_shape=out_shape,
        scratch_shapes=[pltpu.VMEM((tm, d), BF16), pltpu.VMEM((tm, d), F32),
                        pltpu.SemaphoreType.DMA((1,))],
        compiler_params=pltpu.CompilerParams(
            dimension_semantics=("arbitrary", "arbitrary"),
            vmem_limit_bytes=_vmem_limit(blocks, scratch, SPILL_ALLOWANCE_BYTES),
        ),
        name=f"ffn{layer}",
    )(h, ada, g, w_in, w_in, w_out, g2)


def _mix_in_kernel(n_ref, w_ref, q_ref, wbf_ref, *, raw_blocks):
    j = pl.program_id(0)

    @pl.when(pl.program_id(1) == 0)
    def _():
        wbf_ref[...] = w_ref[...].astype(BF16)

    pa = _dot(n_ref[...], wbf_ref[...])
    raw = functools.reduce(jnp.logical_or, [j == b for b in raw_blocks])
    q_ref[...] = jnp.where(raw, pa, _sigmoid(pa))


def _mix_in_call(n, w_in, *, tm, tn, raw_blocks):
    t, d = n.shape
    n_cols = w_in.shape[1]
    kern = functools.partial(_mix_in_kernel, raw_blocks=raw_blocks)
    blocks = _nbytes((tm, d), BF16) + _nbytes((d, tn), F32) + _nbytes((tm, tn), F32)
    scratch = _nbytes((d, tn), BF16)
    temps = 3 * _nbytes((tm, tn), F32)
    return pl.pallas_call(
        kern,
        grid=(n_cols // tn, t // tm),
        in_specs=[
            pl.BlockSpec((tm, d), lambda j, i: (i, 0)),
            pl.BlockSpec((d, tn), lambda j, i: (0, j)),
        ],
        out_specs=pl.BlockSpec((tm, tn), lambda j, i: (i, j)),
        out_shape=jax.ShapeDtypeStruct((t, n_cols), F32),
        scratch_shapes=[pltpu.VMEM((d, tn), BF16)],
        compiler_params=pltpu.CompilerParams(
            dimension_semantics=("arbitrary", "arbitrary"),
            vmem_limit_bytes=_vmem_limit(blocks, scratch, temps),
        ),
        name="mix_in",
    )(n, w_in)


CONV_STEPS = 16


def _to_time_major(dst_ref, row0, src, n_rows):
    for cb in range(src.shape[1] // LANES):
        dst_ref[pl.ds(row0 * SUBLANES_F32 + cb, n_rows, stride=SUBLANES_F32), :] = (
            src[:, cb * LANES:(cb + 1) * LANES])


def _from_time_major(src_ref, n_rows, cb_lo, cb_hi):
    return jnp.concatenate(
        [src_ref[pl.ds(cb, n_rows, stride=SUBLANES_F32), :] for cb in range(cb_lo, cb_hi)],
        axis=-1)


def _mix_body_kernel(av_ref, as_ref, avh_ref, ash_ref, p_ref, ph_ref,
                     ga0_ref, ga1_ref, gb0_ref, gb1_ref, h_ref, ada_ref,
                     cw_ref, cb_ref, lng_ref, lnb_ref, wa_ref, ba_ref,
                     wb_ref, bb_ref, ls_ref, wo_ref, o_ref,
                     a2d_ref, c2d_ref, p2d_ref, m2d_ref, z_ref, *, tiles_per_seq):
    i = pl.program_id(0)
    tm, d = h_ref.shape
    w_conv = av_ref.shape[1]
    n_cb = w_conv // LANES
    seq_tile = i % tiles_per_seq
    seq_start = seq_tile == 0

    a_hist = jnp.where(seq_start, 0.0, avh_ref[...] * ash_ref[...])
    _to_time_major(a2d_ref, 0, a_hist, CONV_HALO)
    _to_time_major(a2d_ref, CONV_HALO, av_ref[...] * as_ref[...], tm)
    p_hist = jnp.where(seq_start, 0.0, ph_ref[...])
    _to_time_major(p2d_ref, 0, p_hist, POOL_HALO)
    _to_time_major(p2d_ref, POOL_HALO, p_ref[...], tm)

    base = CONV_HALO - (CONV_K - 1)

    def conv_group(gi, carry):
        t0 = gi * CONV_STEPS
        accs = [None] * CONV_STEPS
        for k in range(CONV_K):
            wk = cw_ref[k * SUBLANES_F32:(k + 1) * SUBLANES_F32, :]
            for s in range(CONV_STEPS):
                row = pl.multiple_of((t0 + s + base + k) * SUBLANES_F32, SUBLANES_F32)
                term = a2d_ref[pl.ds(row, SUBLANES_F32), :] * wk
                accs[s] = term if accs[s] is None else accs[s] + term
        for s in range(CONV_STEPS):
            row = pl.multiple_of((t0 + s) * SUBLANES_F32, SUBLANES_F32)
            c2d_ref[pl.ds(row, SUBLANES_F32), :] = accs[s] + cb_ref[...]
        return carry

    jax.lax.fori_loop(0, tm // CONV_STEPS, conv_group, 0)

    cv = _from_time_major(c2d_ref, tm, 0, n_cb)
    mu = jnp.mean(cv, axis=-1, keepdims=True)
    dv = cv - mu
    var = jnp.mean(dv * dv, axis=-1, keepdims=True)
    y = (dv * jax.lax.rsqrt(var + EPS)) * lng_ref[...] + lnb_ref[...]
    ya = _dot(_silu(y).astype(BF16), wa_ref[...]) + ba_ref[...]

    def lag(j):
        return p2d_ref[pl.ds((POOL_HALO - j) * SUBLANES_F32, tm * SUBLANES_F32), :]

    v = lag(0)
    s2 = v + lag(1)
    s4 = s2 + (lag(2) + lag(3))
    s8 = s4 + ((lag(4) + lag(5)) + (lag(6) + lag(7)))
    s16 = s8 + (((lag(8) + lag(9)) + (lag(10) + lag(11)))
                + ((lag(12) + lag(13)) + (lag(14) + lag(15))))
    row = jax.lax.broadcasted_iota(jnp.int32, (tm * SUBLANES_F32, LANES), 0)
    group = jax.lax.shift_right_logical(jnp.bitwise_and(row, SUBLANES_F32 - 1), 1)
    pos = seq_tile * tm + jax.lax.shift_right_logical(row, 3)
    window = jnp.left_shift(2, group)
    count = jnp.minimum(pos + 1, window).astype(F32)
    wsum = jnp.where(group == 0, s2, jnp.where(group == 1, s4, jnp.where(group == 2, s8, s16)))
    m2d_ref[...] = wsum / count - v

    n_groups = len(POOL_WINDOWS)
    gout = d // n_groups
    cb_per_group = p_ref.shape[1] // LANES // n_groups
    gate_a = (ga0_ref, ga1_ref)
    gate_b = (gb0_ref, gb1_ref)
    half = d // 2
    for g in range(n_groups):
        cout = slice(g * gout, (g + 1) * gout)
        gsrc = slice((g * gout) % half, (g * gout) % half + gout)
        mixed = _from_time_major(m2d_ref, tm, g * cb_per_group, (g + 1) * cb_per_group)
        yb = (_dot(mixed.astype(BF16), wb_ref[g]) + bb_ref[g:g + 1, :]) * ls_ref[:, cout]
        ga = gate_a[(g * gout) // half][:, gsrc]
        gb = gate_b[(g * gout) // half][:, gsrc]
        z_ref[:, cout] = (ga * ya[:, cout] + gb * yb).astype(BF16)

    mix = _dot(z_ref[...], wo_ref[...])
    o_ref[...] = h_ref[...] + ada_ref[5:6, :] * mix


def _mix_body_call(q, h, ada, conv_w, conv_b, ln_g, ln_b, w_a_out, b_a_out,
                   w_b_group, b_b_group, ls_b, w_out, *, seq, tm, w_conv, w_pool):
    t, d = h.shape
    n_groups, gin, gout = w_b_group.shape
    assert w_conv == w_pool == d // 2 and q.shape[1] == 2 * w_conv + w_pool + 2 * d
    assert w_conv // LANES == SUBLANES_F32
    tiles_per_seq = seq // tm
    a_halo_blocks = tm // CONV_HALO
    p_halo_blocks = tm // POOL_HALO
    const = lambda i: (0, 0)
    resident = pl.Buffered(1)
    kern = functools.partial(_mix_body_kernel, tiles_per_seq=tiles_per_seq)
    wcol = w_conv

    def qblock(col):
        return pl.BlockSpec((tm, wcol), lambda i: (i, col))

    blocks = 7 * _nbytes((tm, wcol), F32) + 2 * _nbytes((tm, d), F32) + _nbytes((128, d), F32)
    weights = (_nbytes((w_conv, d), BF16) + _nbytes((n_groups, gin, gout), BF16)
               + _nbytes((d, d), BF16))
    scratch = (_nbytes((tm + CONV_HALO, w_conv), F32) + _nbytes((tm + POOL_HALO, w_pool), F32)
               + 2 * _nbytes((tm, w_conv), F32) + _nbytes((tm, d), BF16))
    temps = 4 * _nbytes((tm, d), F32)
    return pl.pallas_call(
        kern,
        grid=(t // tm,),
        in_specs=[
            qblock(0),
            qblock(1),
            pl.BlockSpec((CONV_HALO, wcol), lambda i: (jnp.maximum(i * a_halo_blocks - 1, 0), 0)),
            pl.BlockSpec((CONV_HALO, wcol), lambda i: (jnp.maximum(i * a_halo_blocks - 1, 0), 1)),
            qblock(2),
            pl.BlockSpec((POOL_HALO, wcol), lambda i: (jnp.maximum(i * p_halo_blocks - 1, 0), 2)),
            qblock(3),
            qblock(4),
            qblock(5),
            qblock(6),
            pl.BlockSpec((tm, d), lambda i: (i, 0)),
            pl.BlockSpec((None, N_ADA * 3, d), lambda i: (i // tiles_per_seq, 0, 0)),
            pl.BlockSpec((CONV_K * SUBLANES_F32, LANES), const),
            pl.BlockSpec((SUBLANES_F32, LANES), const),
            pl.BlockSpec((1, w_conv), const),
            pl.BlockSpec((1, w_conv), const),
            pl.BlockSpec((w_conv, d), const, pipeline_mode=resident),
            pl.BlockSpec((1, d), const),
            pl.BlockSpec((n_groups, gin, gout), lambda i: (0, 0, 0), pipeline_mode=resident),
            pl.BlockSpec((n_groups, gout), const),
            pl.BlockSpec((1, d), const),
            pl.BlockSpec((d, d), const, pipeline_mode=resident),
        ],
        out_specs=pl.BlockSpec((tm, d), lambda i: (i, 0)),
        out_shape=jax.ShapeDtypeStruct((t, d), F32),
        scratch_shapes=[
            pltpu.VMEM(((tm + CONV_HALO) * SUBLANES_F32, LANES), F32),
            pltpu.VMEM((tm * SUBLANES_F32, LANES), F32),
            pltpu.VMEM(((tm + POOL_HALO) * SUBLANES_F32, LANES), F32),
            pltpu.VMEM((tm * SUBLANES_F32, LANES), F32),
            pltpu.VMEM((tm, d), BF16),
        ],
        compiler_params=pltpu.CompilerParams(
            dimension_semantics=("arbitrary",),
            vmem_limit_bytes=_vmem_limit(blocks, weights + scratch, temps),
        ),
        name="mix_body",
    )(q, q, q, q, q, q, q, q, q, q, h, ada,
      conv_w.reshape(CONV_K * SUBLANES_F32, LANES), conv_b.reshape(SUBLANES_F32, LANES),
      ln_g, ln_b, w_a_out, b_a_out, w_b_group, b_b_group, ls_b, w_out)


FFN_TM = 1024
FFN_TF = 256
MIX_IN_TM = 1024
MIX_IN_TN = 1024
MIX_BODY_TM = 256
ADA_TN = 2048


def kernel(x, c, w_ada, b_ada, g_ffn1, w1_in, w1_out, g_mix, w_in, conv_w, conv_b,
           ln_a_g, ln_a_b, w_a_out, b_a_out, w_b_group, b_b_group, ls_b, w_out,
           g_ffn2, w2_in, w2_out, g_final):
    bsz, seq, d = x.shape
    w_conv = conv_w.shape[1]
    w_pool = w_b_group.shape[0] * w_b_group.shape[1]
    row = lambda v: v.reshape(1, -1)

    c_pad = jnp.pad(c, ((0, SUBLANES_BF16 - bsz), (0, 0)))
    ada = _ada_call(c_pad, w_ada, row(b_ada), tn=ADA_TN)[:bsz]
    ada = ada.reshape(bsz, N_ADA * 3, d)

    h = x.reshape(bsz * seq, d)
    h, n_mix = _ffn_call(h, ada, row(g_ffn1), w1_in, w1_out, row(g_mix),
                         layer=0, epilogue="next_norm", seq=seq, tm=FFN_TM, tf=FFN_TF)

    raw_blocks = (0, (2 * w_conv) // MIX_IN_TN)
    q = _mix_in_call(n_mix, w_in, tm=MIX_IN_TM, tn=MIX_IN_TN, raw_blocks=raw_blocks)
    h = _mix_body_call(q, h, ada, conv_w, conv_b, row(ln_a_g), row(ln_a_b),
                       w_a_out.astype(BF16), row(b_a_out), w_b_group.astype(BF16), b_b_group,
                       row(ls_b), w_out.astype(BF16), seq=seq, tm=MIX_BODY_TM,
                       w_conv=w_conv, w_pool=w_pool)

    (h,) = _ffn_call(h, ada, row(g_ffn2), w2_in, w2_out, row(g_final),
                     layer=2, epilogue="final_norm", seq=seq, tm=FFN_TM, tf=FFN_TF)
    return h.reshape(bsz, seq, d)
```

```python
import functools

import jax
import jax.numpy as jnp
from jax.experimental import pallas as pl
from jax.experimental.pallas import tpu as pltpu

EPS = 1e-6
N_ADA = 3
CONV_K = 31
POOL_WINDOWS = (2, 4, 8, 16)

V7X_VMEM_BYTES = 64 * 1024 * 1024
V7X_VMEM_RESERVE_BYTES = 4 * 1024 * 1024
SPILL_ALLOWANCE_BYTES = 8 * 1024 * 1024
LANES = 128
SUBLANES_F32 = 8
SUBLANES_BF16 = 16

CONV_HALO = 32
POOL_HALO = 16

BF16 = jnp.bfloat16
F32 = jnp.float32


def _vmem_limit(pipelined_bytes, scratch_bytes, temp_bytes):
    want = 2 * pipelined_bytes + scratch_bytes + temp_bytes
    return int(min(want, V7X_VMEM_BYTES - V7X_VMEM_RESERVE_BYTES))


def _nbytes(shape, dtype):
    n = 1
    for s in shape:
        n *= s
    return n * jnp.dtype(dtype).itemsize


def _sigmoid(x):
    return 0.5 * jnp.tanh(0.5 * x) + 0.5


def _silu(x):
    return x * _sigmoid(x)


def _rms_norm(x, g):
    ms = jnp.mean(x * x, axis=-1, keepdims=True)
    return (x * jax.lax.rsqrt(ms + EPS)) * g


def _dot(a, b):
    return jnp.dot(a, b, preferred_element_type=F32)


def _ada_kernel(c_ref, w_ref, b_ref, o_ref):
    s = _silu(c_ref[...]).astype(BF16)
    o_ref[...] = _dot(s, w_ref[...].astype(BF16)) + b_ref[...]


def _ada_call(c_pad, w_ada, b_ada, *, tn):
    rows, d = c_pad.shape
    n_out = w_ada.shape[1]
    blocks = _nbytes((d, tn), F32) + _nbytes((rows, tn), F32) * 2 + _nbytes((rows, d), F32)
    return pl.pallas_call(
        _ada_kernel,
        grid=(n_out // tn,),
        in_specs=[
            pl.BlockSpec((rows, d), lambda j: (0, 0)),
            pl.BlockSpec((d, tn), lambda j: (0, j)),
            pl.BlockSpec((1, tn), lambda j: (0, j)),
        ],
        out_specs=pl.BlockSpec((rows, tn), lambda j: (0, j)),
        out_shape=jax.ShapeDtypeStruct((rows, n_out), F32),
        compiler_params=pltpu.CompilerParams(
            dimension_semantics=("arbitrary",),
            vmem_limit_bytes=_vmem_limit(blocks, 0, _nbytes((d, tn), BF16)),
        ),
        name="ada_proj",
    )(c_pad, w_ada, b_ada)


FFN_ROW_CHUNKS = 4
FFN_OUT_LANES = 512


def _ffn_kernel(h_hbm, ada_ref, g_ref, wh_ref, wu_ref, wo_ref, g2_ref, o_ref, *rest,
                ada_row, epilogue):
    if epilogue == "next_norm":
        nnext_ref, n_ref, x_ref, sem = rest
    else:
        n_ref, x_ref, sem = rest
    i = pl.program_id(0)
    f = pl.program_id(1)
    last = pl.num_programs(1) - 1
    tm, d = o_ref.shape
    rc = tm // FFN_ROW_CHUNKS
    chunks = [slice(c * rc, (c + 1) * rc) for c in range(FFN_ROW_CHUNKS)]

    def fetch(tile):
        return pltpu.make_async_copy(h_hbm.at[pl.ds(tile * tm, tm), :], x_ref, sem.at[0])

    def weights():
        return (wh_ref[...].astype(BF16), wu_ref[...].astype(BF16), wo_ref[...].astype(BF16))

    def normed(rows):
        shift = ada_ref[ada_row:ada_row + 1, :]
        scale = ada_ref[ada_row + 1:ada_row + 2, :]
        n = (_rms_norm(x_ref[rows, :], g_ref[...]) * (1.0 + scale) + shift).astype(BF16)
        n_ref[rows, :] = n
        return n

    def branch(rows, n, wts, base_ref):
        whb, wub, wob = wts
        act = (_silu(_dot(n, whb)) * _dot(n, wub)).astype(BF16)
        half_gate = 0.5 * ada_ref[ada_row + 2:ada_row + 3, :]
        for c0 in range(0, d, FFN_OUT_LANES):
            cs = slice(c0, c0 + FFN_OUT_LANES)
            o_ref[rows, cs] = base_ref[rows, cs] + half_gate[:, cs] * _dot(act, wob[:, cs])

    def finish(rows):
        y = o_ref[rows, :]
        if epilogue == "next_norm":
            shift = ada_ref[ada_row + 3:ada_row + 4, :]
            scale = ada_ref[ada_row + 4:ada_row + 5, :]
            nn = _rms_norm(y, g2_ref[...]) * (1.0 + scale) + shift
            nnext_ref[rows, :] = nn.astype(BF16)
        else:
            o_ref[rows, :] = _rms_norm(y, g2_ref[...])

    @pl.when(f == 0)
    def _():
        @pl.when(i == 0)
        def _():
            fetch(0).start()
        fetch(i).wait()
        wts = weights()
        n = normed(chunks[0])
        for c in range(FFN_ROW_CHUNKS):
            branch(chunks[c], n, wts, x_ref)
            if c + 1 < FFN_ROW_CHUNKS:
                n = normed(chunks[c + 1])

    @pl.when(jnp.logical_and(f == 1, i + 1 < pl.num_programs(0)))
    def _():
        fetch(i + 1).start()

    @pl.when(jnp.logical_and(f > 0, f < last))
    def _():
        branch(slice(None), n_ref[...], weights(), o_ref)

    @pl.when(f == last)
    def _():
        wts = weights()
        for rows in chunks:
            branch(rows, n_ref[rows, :], wts, o_ref)
            finish(rows)


def _ffn_call(h, ada, g, w_in, w_out, g2, *, layer, epilogue, seq, tm, tf):
    t, d = h.shape
    d_ff = w_out.shape[0]
    tiles_per_seq = seq // tm
    n_f = d_ff // tf
    assert tm % (FFN_ROW_CHUNKS * SUBLANES_BF16) == 0 and d % FFN_OUT_LANES == 0 and n_f >= 3
    kern = functools.partial(_ffn_kernel, ada_row=3 * layer, epilogue=epilogue)
    out_specs = [pl.BlockSpec((tm, d), lambda i, f: (i, 0))]
    out_shape = [jax.ShapeDtypeStruct((t, d), F32)]
    blocks = (_nbytes((tm, d), F32) + _nbytes((N_ADA * 3, d), F32) + 2 * _nbytes((1, d), F32)
              + 3 * _nbytes((d, tf), w_in.dtype))
    if epilogue == "next_norm":
        out_specs.append(pl.BlockSpec((tm, d), lambda i, f: (i, 0)))
        out_shape.append(jax.ShapeDtypeStruct((t, d), BF16))
        blocks += _nbytes((tm, d), BF16)
    scratch = _nbytes((tm, d), BF16) + _nbytes((tm, d), F32)
    return pl.pallas_call(
        kern,
        grid=(t // tm, n_f),
        in_specs=[
            pl.BlockSpec(memory_space=pl.ANY),
            pl.BlockSpec((None, N_ADA * 3, d), lambda i, f: (i // tiles_per_seq, 0, 0)),
            pl.BlockSpec((1, d), lambda i, f: (0, 0)),
            pl.BlockSpec((d, tf), lambda i, f: (0, f)),
            pl.BlockSpec((d, tf), lambda i, f: (0, f + n_f)),
            pl.BlockSpec((tf, d), lambda i, f: (f, 0)),
            pl.BlockSpec((1, d), lambda i, f: (0, 0)),
        ],
        out_specs=out_specs,
        out_shape=out_shape,
        scratch_shapes=[pltpu.VMEM((tm, d), BF16), pltpu.VMEM((tm, d), F32),
                        pltpu.SemaphoreType.DMA((1,))],
        compiler_params=pltpu.CompilerParams(
            dimension_semantics=("arbitrary", "arbitrary"),
            vmem_limit_bytes=_vmem_limit(blocks, scratch, SPILL_ALLOWANCE_BYTES),
        ),
        name=f"ffn{layer}",
    )(h, ada, g, w_in, w_in, w_out, g2)


def _mix_in_kernel(n_ref, w_ref, q_ref, wbf_ref, *, raw_blocks):
    j = pl.program_id(0)

    @pl.when(pl.program_id(1) == 0)
    def _():
        wbf_ref[...] = w_ref[...].astype(BF16)

    pa = _dot(n_ref[...], wbf_ref[...])
    raw = functools.reduce(jnp.logical_or, [j == b for b in raw_blocks])
    q_ref[...] = jnp.where(raw, pa, _sigmoid(pa))


def _mix_in_call(n, w_in, *, tm, tn, raw_blocks):
    t, d = n.shape
    n_cols = w_in.shape[1]
    kern = functools.partial(_mix_in_kernel, raw_blocks=raw_blocks)
    blocks = _nbytes((tm, d), BF16) + _nbytes((d, tn), F32) + _nbytes((tm, tn), F32)
    scratch = _nbytes((d, tn), BF16)
    temps = SPILL_ALLOWANCE_BYTES
    return pl.pallas_call(
        kern,
        grid=(n_cols // tn, t // tm),
        in_specs=[
            pl.BlockSpec((tm, d), lambda j, i: (i, 0)),
            pl.BlockSpec((d, tn), lambda j, i: (0, j)),
        ],
        out_specs=pl.BlockSpec((tm, tn), lambda j, i: (i, j)),
        out_shape=jax.ShapeDtypeStruct((t, n_cols), F32),
        scratch_shapes=[pltpu.VMEM((d, tn), BF16)],
        compiler_params=pltpu.CompilerParams(
            dimension_semantics=("arbitrary", "arbitrary"),
            vmem_limit_bytes=_vmem_limit(blocks, scratch, temps),
        ),
        name="mix_in",
    )(n, w_in)


CONV_STEPS = 16
CONV_CHAINS = 4
MIX_ROW_CHUNKS = 4
MIX_OUT_LANES = 256


def _to_time_major(dst_ref, row0, src, n_rows):
    for cb in range(src.shape[1] // LANES):
        dst_ref[pl.ds(row0 * SUBLANES_F32 + cb, n_rows, stride=SUBLANES_F32), :] = (
            src[:, cb * LANES:(cb + 1) * LANES])


def _from_time_major(src_ref, n_rows, cb_lo, cb_hi):
    return jnp.concatenate(
        [src_ref[pl.ds(cb, n_rows, stride=SUBLANES_F32), :] for cb in range(cb_lo, cb_hi)],
        axis=-1)


def _mix_body_kernel(vsp_ref, hist_ref,
                     ga0_ref, ga1_ref, gb0_ref, gb1_ref, h_ref, ada_ref,
                     cw_ref, cb_ref, lng_ref, lnb_ref, wa_ref, ba_ref,
                     wb_ref, bb_ref, ls_ref, wo_ref, o_ref,
                     a2d_ref, c2d_ref, p2d_ref, m2d_ref, w2_ref, w4_ref, w8_ref,
                     sa_ref, mx_ref, z_ref, *, tiles_per_seq, n_tiles):
    s = pl.program_id(0)
    tm, d = h_ref.shape
    w_conv = wa_ref.shape[0]
    w_pool = vsp_ref.shape[1] - 2 * w_conv
    n_cb = w_conv // LANES
    n_groups = len(POOL_WINDOWS)
    gin = w_pool // n_groups
    gout = d // n_groups
    val_cols = slice(0, w_conv)
    sig_cols = slice(w_conv, 2 * w_conv)
    pool_cols = slice(2 * w_conv, 2 * w_conv + w_pool)
    slot_in = s % 2
    slot_out = 1 - slot_in
    seq_tile = jnp.minimum(s, n_tiles - 1) % tiles_per_seq
    seq_start = seq_tile == 0

    @pl.when(s == 0)
    def _():
        sa_ref[1] = jnp.zeros(sa_ref.shape[1:], BF16)
        mx_ref[1] = jnp.zeros(mx_ref.shape[1:], BF16)

    gate_a = (ga0_ref, ga1_ref)
    gate_b = (gb0_ref, gb1_ref)
    half = d // 2
    rchunk = tm // MIX_ROW_CHUNKS

    def branch_piece(g):
        def run():
            cout = slice(g * gout, (g + 1) * gout)
            gsrc = slice((g * gout) % half, (g * gout) % half + gout)
            ya = _dot(sa_ref[slot_out], wa_ref[:, cout]) + ba_ref[:, cout]
            mixed = mx_ref[slot_out, :, g * gin:(g + 1) * gin]
            yb = (_dot(mixed, wb_ref[g]) + bb_ref[g:g + 1, :]) * ls_ref[:, cout]
            ga = gate_a[(g * gout) // half][:, gsrc]
            gb = gate_b[(g * gout) // half][:, gsrc]
            z_ref[:, cout] = (ga * ya + gb * yb).astype(BF16)
        return run

    def out_piece(c):
        def run():
            cs = slice(c * MIX_OUT_LANES, (c + 1) * MIX_OUT_LANES)
            mix = _dot(z_ref[...], wo_ref[:, cs])
            o_ref[:, cs] = h_ref[:, cs] + ada_ref[5:6, cs] * mix
        return run

    def fill_piece():
        a_hist = jnp.where(seq_start, 0.0, hist_ref[:, val_cols] * hist_ref[:, sig_cols])
        _to_time_major(a2d_ref, 0, a_hist, CONV_HALO)
        _to_time_major(a2d_ref, CONV_HALO, vsp_ref[:, val_cols] * vsp_ref[:, sig_cols], tm)
        p_hist = jnp.where(seq_start, 0.0, hist_ref[CONV_HALO - POOL_HALO:, pool_cols])
        _to_time_major(p2d_ref, 0, p_hist, POOL_HALO)
        _to_time_major(p2d_ref, POOL_HALO, vsp_ref[:, pool_cols], tm)

    base = CONV_HALO - (CONV_K - 1)

    chain_tail = [None] * CONV_CHAINS

    def zero_words(x):
        bits = pltpu.bitcast(x, jnp.uint32)
        return jnp.right_shift(jnp.right_shift(bits, jnp.uint32(16)), jnp.uint32(16))

    def conv_piece(t0):
        def run():
            for g0 in range(t0, t0 + CONV_STEPS, CONV_CHAINS):
                accs = [None] * CONV_CHAINS
                for k in range(CONV_K):
                    wk = cw_ref[k * SUBLANES_F32:(k + 1) * SUBLANES_F32, :]
                    for j in range(CONV_CHAINS):
                        row = (g0 + j + base + k) * SUBLANES_F32
                        term = a2d_ref[row:row + SUBLANES_F32, :] * wk
                        if k == 0 and chain_tail[j] is not None:
                            term = jnp.where(chain_tail[j] == 0, term, 0.0)
                        accs[j] = term if accs[j] is None else accs[j] + term
                for j in range(CONV_CHAINS):
                    row = (g0 + j) * SUBLANES_F32
                    c2d_ref[row:row + SUBLANES_F32, :] = accs[j] + cb_ref[...]
                    chain_tail[j] = zero_words(accs[j])
        return run

    def norm_piece(r0):
        def run():
            cv = jnp.concatenate(
                [c2d_ref[pl.ds(r0 * SUBLANES_F32 + cb, rchunk, stride=SUBLANES_F32), :]
                 for cb in range(n_cb)], axis=-1)
            mu = jnp.mean(cv, axis=-1, keepdims=True)
            dv = cv - mu
            var = jnp.mean(dv * dv, axis=-1, keepdims=True)
            y = (dv * jax.lax.rsqrt(var + EPS)) * lng_ref[...] + lnb_ref[...]
            sa_ref[slot_in, r0:r0 + rchunk, :] = _silu(y).astype(BF16)
        return run

    def pool_piece(r0):
        def run():
            r8 = SUBLANES_F32

            def rows(ref, lo, cnt):
                return ref[lo * r8:(lo + cnt) * r8, :]

            p_lo = POOL_HALO + r0
            n = rchunk
            w2_ref[0:(n + 14) * r8, :] = rows(p2d_ref, p_lo - 14, n + 14) + rows(p2d_ref, p_lo - 15, n + 14)
            w4_ref[0:(n + 12) * r8, :] = rows(w2_ref, 2, n + 12) + rows(w2_ref, 0, n + 12)
            w8_ref[0:(n + 8) * r8, :] = rows(w4_ref, 4, n + 8) + rows(w4_ref, 0, n + 8)
            w16 = rows(w8_ref, 8, n) + rows(w8_ref, 0, n)

            sub = jax.lax.broadcasted_iota(jnp.int32, (r8, LANES), 0)
            group = jax.lax.shift_right_logical(sub, 1)
            window = jnp.left_shift(2, group)
            inv_window = 1.0 / window.astype(F32)
            for j in range(n):
                v = rows(p2d_ref, p_lo + j, 1)
                wsum = jnp.where(group == 0, rows(w2_ref, 14 + j, 1),
                                 jnp.where(group == 1, rows(w4_ref, 12 + j, 1),
                                           jnp.where(group == 2, rows(w8_ref, 8 + j, 1),
                                                     w16[j * r8:(j + 1) * r8, :])))
                if r0 + j + 1 < max(POOL_WINDOWS):
                    count = jnp.minimum(seq_tile * tm + (r0 + j + 1), window).astype(F32)
                    pooled = wsum / count
                else:
                    pooled = wsum * inv_window
                m2d_ref[(r0 + j) * r8:(r0 + j + 1) * r8, :] = pooled - v
            mixed = jnp.concatenate(
                [m2d_ref[pl.ds(r0 * r8 + cb, rchunk, stride=r8), :]
                 for cb in range(w_pool // LANES)], axis=-1)
            mx_ref[slot_in, r0:r0 + rchunk, :] = mixed.astype(BF16)
        return run

    matmul_pieces = ([branch_piece(g) for g in range(n_groups)]
                     + [out_piece(c) for c in range(d // MIX_OUT_LANES)])
    vector_pieces = [conv_piece(t0) for t0 in range(0, tm, CONV_STEPS)]
    for r0 in range(0, tm, rchunk):
        vector_pieces += [norm_piece(r0), pool_piece(r0)]
    fill_piece()
    n_m, n_v = len(matmul_pieces), len(vector_pieces)
    for m, run in enumerate(matmul_pieces):
        run()
        for piece in vector_pieces[m * n_v // n_m:(m + 1) * n_v // n_m]:
            piece()


def _mix_body_call(q, h, ada, conv_w, conv_b, ln_g, ln_b, w_a_out, b_a_out,
                   w_b_group, b_b_group, ls_b, w_out, *, seq, tm, w_conv, w_pool):
    t, d = h.shape
    n_groups, gin, gout = w_b_group.shape
    assert w_conv == w_pool == d // 2 and q.shape[1] == 2 * w_conv + w_pool + 2 * d
    assert w_conv // LANES == SUBLANES_F32
    tiles_per_seq = seq // tm
    const = lambda s: (0, 0)
    resident = pl.Buffered(1)
    n_tiles = t // tm
    kern = functools.partial(_mix_body_kernel, tiles_per_seq=tiles_per_seq, n_tiles=n_tiles)
    wcol = w_conv

    def cur(s):
        return jnp.minimum(s, n_tiles - 1)

    def prev(s):
        return jnp.maximum(s - 1, 0)

    def qblock(col, tile):
        return pl.BlockSpec((tm, wcol), lambda s: (tile(s), col))

    vsp_cols = 2 * w_conv + w_pool
    halos_per_tile = tm // CONV_HALO
    vsp_spec = pl.BlockSpec((tm, vsp_cols), lambda s: (cur(s), 0))
    hist_spec = pl.BlockSpec((CONV_HALO, vsp_cols),
                             lambda s: (jnp.maximum(cur(s) * halos_per_tile - 1, 0), 0))

    pool_tmp = ((tm // MIX_ROW_CHUNKS + POOL_HALO) * SUBLANES_F32, LANES)
    blocks = 7 * _nbytes((tm, wcol), F32) + 2 * _nbytes((tm, d), F32) + _nbytes((128, d), F32)
    weights = (_nbytes((w_conv, d), BF16) + _nbytes((n_groups, gin, gout), BF16)
               + _nbytes((d, d), BF16))
    scratch = (_nbytes((tm + CONV_HALO, w_conv), F32) + _nbytes((tm + POOL_HALO, w_pool), F32)
               + 2 * _nbytes((tm, w_conv), F32) + _nbytes((tm, d), BF16)
               + 2 * _nbytes((2, tm, w_conv), BF16) + 3 * _nbytes(pool_tmp, F32))
    return pl.pallas_call(
        kern,
        grid=(n_tiles + 1,),
        in_specs=[
            vsp_spec,
            hist_spec,
            qblock(3, prev),
            qblock(4, prev),
            qblock(5, prev),
            qblock(6, prev),
            pl.BlockSpec((tm, d), lambda s: (prev(s), 0)),
            pl.BlockSpec((None, N_ADA * 3, d), lambda s: (prev(s) // tiles_per_seq, 0, 0)),
            pl.BlockSpec((CONV_K * SUBLANES_F32, LANES), const),
            pl.BlockSpec((SUBLANES_F32, LANES), const),
            pl.BlockSpec((1, w_conv), const),
            pl.BlockSpec((1, w_conv), const),
            pl.BlockSpec((w_conv, d), const, pipeline_mode=resident),
            pl.BlockSpec((1, d), const),
            pl.BlockSpec((n_groups, gin, gout), lambda i: (0, 0, 0), pipeline_mode=resident),
            pl.BlockSpec((n_groups, gout), const),
            pl.BlockSpec((1, d), const),
            pl.BlockSpec((d, d), const, pipeline_mode=resident),
        ],
        out_specs=pl.BlockSpec((tm, d), lambda s: (prev(s), 0)),
        out_shape=jax.ShapeDtypeStruct((t, d), F32),
        scratch_shapes=[
            pltpu.VMEM(((tm + CONV_HALO) * SUBLANES_F32, LANES), F32),
            pltpu.VMEM((tm * SUBLANES_F32, LANES), F32),
            pltpu.VMEM(((tm + POOL_HALO) * SUBLANES_F32, LANES), F32),
            pltpu.VMEM((tm * SUBLANES_F32, LANES), F32),
            pltpu.VMEM(pool_tmp, F32),
            pltpu.VMEM(pool_tmp, F32),
            pltpu.VMEM(pool_tmp, F32),
            pltpu.VMEM((2, tm, w_conv), BF16),
            pltpu.VMEM((2, tm, w_pool), BF16),
            pltpu.VMEM((tm, d), BF16),
        ],
        compiler_params=pltpu.CompilerParams(
            dimension_semantics=("arbitrary",),
            vmem_limit_bytes=_vmem_limit(blocks, weights + scratch, SPILL_ALLOWANCE_BYTES),
        ),
        name="mix_body",
    )(q, q, q, q, q, q, h, ada,
      conv_w.reshape(CONV_K * SUBLANES_F32, LANES), conv_b.reshape(SUBLANES_F32, LANES),
      ln_g, ln_b, w_a_out, b_a_out, w_b_group, b_b_group, ls_b, w_out)


FFN_TM = 1024
FFN_TF = 256
FFN2_TF = 512
MIX_IN_TM = 1024
MIX_IN_TN = 1024
MIX_BODY_TM = 256
ADA_TN = 2048


def kernel(x, c, w_ada, b_ada, g_ffn1, w1_in, w1_out, g_mix, w_in, conv_w, conv_b,
           ln_a_g, ln_a_b, w_a_out, b_a_out, w_b_group, b_b_group, ls_b, w_out,
           g_ffn2, w2_in, w2_out, g_final):
    bsz, seq, d = x.shape
    w_conv = conv_w.shape[1]
    w_pool = w_b_group.shape[0] * w_b_group.shape[1]
    row = lambda v: v.reshape(1, -1)

    c_pad = jnp.pad(c, ((0, SUBLANES_BF16 - bsz), (0, 0)))
    ada = _ada_call(c_pad, w_ada, row(b_ada), tn=ADA_TN)[:bsz]
    ada = ada.reshape(bsz, N_ADA * 3, d)

    h = x.reshape(bsz * seq, d)
    h, n_mix = _ffn_call(h, ada, row(g_ffn1), w1_in, w1_out, row(g_mix),
                         layer=0, epilogue="next_norm", seq=seq, tm=FFN_TM, tf=FFN_TF)

    raw_blocks = (0, (2 * w_conv) // MIX_IN_TN)
    q = _mix_in_call(n_mix, w_in, tm=MIX_IN_TM, tn=MIX_IN_TN, raw_blocks=raw_blocks)
    h = _mix_body_call(q, h, ada, conv_w, conv_b, row(ln_a_g), row(ln_a_b),
                       w_a_out.astype(BF16), row(b_a_out), w_b_group.astype(BF16), b_b_group,
                       row(ls_b), w_out.astype(BF16), seq=seq, tm=MIX_BODY_TM,
                       w_conv=w_conv, w_pool=w_pool)

    (h,) = _ffn_call(h, ada, row(g_ffn2), w2_in, w2_out, row(g_final),
                     layer=2, epilogue="final_norm", seq=seq, tm=FFN_TM, tf=FFN2_TF)
    return h.reshape(bsz, seq, d)
```
